```python
import jax, jax.numpy as jnp
from jax import lax
import numpy as np

D_MODEL = 2048
BATCH = 4
SEQ = 2048
DEPTH = 4

HEAD_DIM = 128
N_HEADS = D_MODEL // HEAD_DIM
N_SB_HEADS = N_HEADS // 2
N_FOX_HEADS = N_HEADS - N_SB_HEADS
Q_BLOCK = 128
POOL_WINDOWS = (2, 4, 8, 16)
N_POOL_GROUPS = len(POOL_WINDOWS)
POOL_GROUP = D_MODEL // N_POOL_GROUPS
D_FF = ((int(8 * D_MODEL / 3) + 255) // 256) * 256
CONV_WIDTH = 3
N_MOD = 6
EPS = 1e-6
N_ATTN_LAYERS = (DEPTH + 1) // 2
N_POOL_LAYERS = DEPTH // 2

kernel_name = "stickbreak_fox_pool_convffn_adaln"


def rms_norm(x, gain):
    xf = x.astype(jnp.float32)
    y = xf * lax.rsqrt(jnp.mean(xf * xf, axis=-1, keepdims=True) + EPS)
    return (y * gain.astype(jnp.float32)).astype(x.dtype)


def modulate(h, shift, scale):
    return h * (1.0 + scale[:, None, :]) + shift[:, None, :]


def attention_mixer(h, w_in, b_forget, w_out):
    B, S, D = h.shape
    proj = h @ w_in
    qkv = proj[..., :3 * D].reshape(B, S, 3, N_HEADS, HEAD_DIM)
    q = qkv[:, :, 0].transpose(0, 2, 1, 3)
    k = qkv[:, :, 1].transpose(0, 2, 1, 3)
    v = qkv[:, :, 2].transpose(0, 2, 1, 3)
    f_logit = proj[..., 3 * D:].astype(jnp.float32) + b_forget.astype(jnp.float32)
    F = jnp.cumsum(jax.nn.log_sigmoid(f_logit), axis=1).transpose(0, 2, 1)
    nb = S // Q_BLOCK
    q_blocks = q.reshape(B, N_HEADS, nb, Q_BLOCK, HEAD_DIM).transpose(2, 0, 1, 3, 4)
    F_blocks = F.reshape(B, N_FOX_HEADS, nb, Q_BLOCK).transpose(2, 0, 1, 3)
    starts = jnp.arange(nb, dtype=jnp.int32) * Q_BLOCK
    s_idx = jnp.arange(S)
    v_sb, v_fx = v[:, :N_SB_HEADS], v[:, N_SB_HEADS:]
    scale = HEAD_DIM ** -0.5

    def block(args):
        qb, Fq, start = args
        t_idx = start + jnp.arange(Q_BLOCK)
        z = jnp.einsum('bhqd,bhkd->bhqk', qb, k).astype(jnp.float32) * scale
        z_sb, z_fx = z[:, :N_SB_HEADS], z[:, N_SB_HEADS:]
        strict = s_idx[None, :] < t_idx[:, None]
        log_rest = jnp.where(strict, jax.nn.log_sigmoid(-z_sb), 0.0)
        log_after = lax.cumsum(log_rest, axis=3, reverse=True) - log_rest
        w_sb = jnp.where(strict, jnp.exp(jax.nn.log_sigmoid(z_sb) + log_after), 0.0)
        o_sb = jnp.einsum('bhqk,bhkd->bhqd', w_sb.astype(v.dtype), v_sb)
        causal = s_idx[None, :] <= t_idx[:, None]
        logits = z_fx + Fq[..., :, None] - F[:, :, None, :]
        p = jax.nn.softmax(jnp.where(causal, logits, -jnp.inf), axis=-1)
        o_fx = jnp.einsum('bhqk,bhkd->bhqd', p.astype(v.dtype), v_fx)
        return jnp.concatenate([o_sb, o_fx], axis=1)

    out = lax.map(block, (q_blocks, F_blocks, starts))
    out = out.transpose(1, 0, 3, 2, 4).reshape(B, S, D)
    return out @ w_out


def pool_mixer(h, w_pool, pool_scale):
    B, S, D = h.shape
    hf = h.astype(jnp.float32).reshape(B, S, N_POOL_GROUPS, POOL_GROUP)
    cs = jnp.cumsum(hf, axis=1)
    pos = jnp.arange(S)
    diffs = []
    for g, w in enumerate(POOL_WINDOWS):
        c_g = cs[:, :, g]
        lagged = jnp.pad(c_g, ((0, 0), (w, 0), (0, 0)))[:, :S]
        count = jnp.minimum(pos + 1, w).astype(jnp.float32)[None, :, None]
        diffs.append((c_g - lagged) / count - hf[:, :, g])
    d = jnp.stack(diffs, axis=2).astype(h.dtype)
    y = jnp.einsum('bsgc,gce->bsge', d, w_pool).reshape(B, S, D)
    return y * pool_scale


def conv_ffn(h, w_up, conv_w, conv_b, w_down):
    S = h.shape[1]
    u = h @ w_up
    up = jnp.pad(u, ((0, 0), (CONV_WIDTH - 1, 0), (0, 0)))
    y = conv_b
    for i in range(CONV_WIDTH):
        y = y + up[:, i:i + S] * conv_w[i]
    a, g = jnp.split(y, 2, axis=-1)
    return (jax.nn.silu(g) * a) @ w_down


def setup_inputs(seed: int = 0) -> dict:
    key = jax.random.key(seed)
    ks = jax.random.split(key, 16)
    n = jax.random.normal
    D, F2 = D_MODEL, 2 * D_FF
    return {
        "x": n(ks[0], (BATCH, SEQ, D), jnp.float32),
        "c": n(ks[1], (BATCH, D), jnp.float32),
        "w_mod": n(ks[2], (DEPTH, D, N_MOD * D), jnp.float32) * (0.5 * D ** -0.5),
        "b_mod": n(ks[3], (DEPTH, N_MOD * D), jnp.float32) * 0.02,
        "norm_gain": 1.0 + 0.1 * n(ks[4], (DEPTH, 2, D), jnp.float32),
        "w_attn_in": n(ks[5], (N_ATTN_LAYERS, D, 3 * D + N_FOX_HEADS), jnp.float32) * D ** -0.5,
        "b_forget": jax.random.uniform(ks[6], (N_ATTN_LAYERS, N_FOX_HEADS), jnp.float32, 1.0, 4.0),
        "w_attn_out": n(ks[7], (N_ATTN_LAYERS, D, D), jnp.float32) * D ** -0.5,
        "w_pool": n(ks[8], (N_POOL_LAYERS, N_POOL_GROUPS, POOL_GROUP, POOL_GROUP), jnp.float32) * POOL_GROUP ** -0.5,
        "pool_scale": 1.0 + 0.1 * n(ks[9], (N_POOL_LAYERS, D), jnp.float32),
        "w_up": n(ks[10], (DEPTH, D, F2), jnp.float32) * D ** -0.5,
        "conv_w": n(ks[11], (DEPTH, CONV_WIDTH, F2), jnp.float32) * CONV_WIDTH ** -0.5,
        "conv_b": n(ks[12], (DEPTH, F2), jnp.float32) * 0.02,
        "w_down": n(ks[13], (DEPTH, D_FF, D), jnp.float32) * D_FF ** -0.5,
        "final_gain": 1.0 + 0.1 * n(ks[14], (D,), jnp.float32),
    }


def reference(x, c, w_mod, b_mod, norm_gain, w_attn_in, b_forget, w_attn_out, w_pool, pool_scale,
              w_up, conv_w, conv_b, w_down, final_gain):
    cond = jax.nn.silu(c)
    for l in range(DEPTH):
        mod = cond @ w_mod[l] + b_mod[l]
        sh1, sc1, g1, sh2, sc2, g2 = jnp.split(mod, N_MOD, axis=-1)
        h = modulate(rms_norm(x, norm_gain[l, 0]), sh1, sc1)
        i = l // 2
        if l % 2 == 0:
            y = attention_mixer(h, w_attn_in[i], b_forget[i], w_attn_out[i])
        else:
            y = pool_mixer(h, w_pool[i], pool_scale[i])
        x = x + g1[:, None, :] * y
        h = modulate(rms_norm(x, norm_gain[l, 1]), sh2, sc2)
        x = x + g2[:, None, :] * conv_ffn(h, w_up[l], conv_w[l], conv_b[l], w_down[l])
    return rms_norm(x, final_gain)
```

```python
import functools

import jax
import jax.numpy as jnp
from jax import lax
from jax.experimental import pallas as pl
from jax.experimental.pallas import tpu as pltpu

HEAD_DIM = 128
POOL_WINDOWS = (2, 4, 8, 16)
CONV_WIDTH = 3
N_MOD = 6
EPS = 1e-6

V7X_LANES = 128
V7X_SUBLANES = 8
V7X_VMEM_BYTES = 64 * 1024 * 1024
VMEM_LIMIT_BYTES = 56 * 1024 * 1024

F32 = jnp.float32
BF16 = jnp.bfloat16


def _tile(dim, pref):
    t = min(dim, pref)
    while dim % t:
        t //= 2
    return t


def _params(*sem):
    return pltpu.CompilerParams(dimension_semantics=sem, vmem_limit_bytes=VMEM_LIMIT_BYTES)


def _rms(x, gain):
    ms = jnp.mean(x * x, axis=-1, keepdims=True)
    return (x * lax.rsqrt(ms + EPS)) * gain


def _sigmoid(x):
    return 1.0 / (1.0 + jnp.exp(-x))


def _log1pexp_neg_abs(z):
    return jnp.log(1.0 + jnp.exp(-jnp.abs(z)))


def _mod_kernel(c_ref, w_ref, b_ref, o_ref):
    c = c_ref[...]
    cond = c * _sigmoid(c)
    o_ref[...] = jnp.dot(cond.astype(BF16), w_ref[...].astype(BF16),
                         preferred_element_type=F32) + b_ref[...]


def _mod_all(c, w_mod, b_mod):
    depth, d, n = w_mod.shape
    b = c.shape[0]
    rows = -(-b // V7X_SUBLANES) * V7X_SUBLANES
    c_pad = jnp.pad(c, ((0, rows - b), (0, 0)))
    tn = _tile(n, 1024)
    out = pl.pallas_call(
        _mod_kernel,
        grid=(depth, n // tn),
        in_specs=[
            pl.BlockSpec((rows, d), lambda l, j: (0, 0)),
            pl.BlockSpec((None, d, tn), lambda l, j: (l, 0, j)),
            pl.BlockSpec((None, 1, tn), lambda l, j: (l, 0, j)),
        ],
        out_specs=pl.BlockSpec((None, rows, tn), lambda l, j: (l, 0, j)),
        out_shape=jax.ShapeDtypeStruct((depth, rows, n), F32),
        compiler_params=_params("parallel", "parallel"),
        name="adaln_mod",
    )(c_pad, w_mod, b_mod.reshape(depth, 1, n))
    return out[:, :b].reshape(depth, b, N_MOD, d)


def _prenorm_kernel(x_ref, gain_ref, mod_ref, h_ref):
    h = _rms(x_ref[...], gain_ref[0:1, :]) * (1.0 + mod_ref[1:2, :]) + mod_ref[0:1, :]
    h_ref[...] = h.astype(h_ref.dtype)


def _prenorm(x2, gain2, mod_l, seq):
    t, d = x2.shape
    tm = _tile(seq, 512)
    per_seq = seq // tm
    return pl.pallas_call(
        _prenorm_kernel,
        grid=(t // tm,),
        in_specs=[
            pl.BlockSpec((tm, d), lambda i: (i, 0)),
            pl.BlockSpec((2, d), lambda i: (0, 0)),
            pl.BlockSpec((None, N_MOD, d), lambda i: (i // per_seq, 0, 0)),
        ],
        out_specs=pl.BlockSpec((tm, d), lambda i: (i, 0)),
        out_shape=jax.ShapeDtypeStruct((t, d), BF16),
        compiler_params=_params("parallel"),
        name="prenorm",
    )(x2, gain2, mod_l)


def _mm_kernel(h_ref, w_ref, o_ref):
    o_ref[...] = jnp.dot(h_ref[...], w_ref[...], preferred_element_type=F32).astype(o_ref.dtype)


def _matmul_cols(h, w, n_out):
    t, k = h.shape
    tm = _tile(t, 1024)
    tn = _tile(n_out, 1024)
    return pl.pallas_call(
        _mm_kernel,
        grid=(n_out // tn, t // tm),
        in_specs=[
            pl.BlockSpec((tm, k), lambda j, i: (i, 0)),
            pl.BlockSpec((k, tn), lambda j, i: (0, j)),
        ],
        out_specs=pl.BlockSpec((tm, tn), lambda j, i: (i, j)),
        out_shape=jax.ShapeDtypeStruct((t, n_out), BF16),
        compiler_params=_params("parallel", "parallel"),
        name="qkv_proj",
    )(h, w)


def _forget_kernel(h_ref, wf_ref, b_ref, o_ref):
    f = jnp.dot(h_ref[...], wf_ref[...], preferred_element_type=F32) + b_ref[...]
    ls = jnp.minimum(f, 0.0) - _log1pexp_neg_abs(f)
    seq = f.shape[0]
    blk = V7X_LANES
    row = lax.broadcasted_iota(jnp.int32, (blk, blk), 0)
    col = lax.broadcasted_iota(jnp.int32, (blk, blk), 1)
    ltri = (col <= row).astype(BF16)
    carry = jnp.zeros((1, ls.shape[1]), F32)
    for i in range(seq // blk):
        x = ls[i * blk:(i + 1) * blk]
        hi = x.astype(BF16)
        r1 = x - hi.astype(F32)
        mid = r1.astype(BF16)
        lo = (r1 - mid.astype(F32)).astype(BF16)
        c = (jnp.dot(ltri, hi, preferred_element_type=F32)
             + jnp.dot(ltri, mid, preferred_element_type=F32)
             + jnp.dot(ltri, lo, preferred_element_type=F32)) + carry
        o_ref[i * blk:(i + 1) * blk, :] = c
        carry = c[blk - 1:blk, :]


def _forget_cumsum(h, wf_pad, bf_pad, batch, seq):
    t, d = h.shape
    return pl.pallas_call(
        _forget_kernel,
        grid=(batch,),
        in_specs=[
            pl.BlockSpec((seq, d), lambda b: (b, 0)),
            pl.BlockSpec((d, V7X_LANES), lambda b: (0, 0)),
            pl.BlockSpec((1, V7X_LANES), lambda b: (0, 0)),
        ],
        out_specs=pl.BlockSpec((seq, V7X_LANES), lambda b: (b, 0)),
        out_shape=jax.ShapeDtypeStruct((t, V7X_LANES), F32),
        compiler_params=_params("parallel"),
        name="forget_cumsum",
    )(h, wf_pad, bf_pad)


ATTN_BLOCK = 128


def _qk(q, kb, scale):
    return lax.dot_general(q, kb, (((1,), (1,)), ((), ())), preferred_element_type=F32) * scale


def _stickbreak_head(q_ref, k_ref, v_ref, o_ref, scale):
    blk = ATTN_BLOCK
    seq = q_ref.shape[0]
    row = lax.broadcasted_iota(jnp.int32, (blk, blk), 0)
    col = lax.broadcasted_iota(jnp.int32, (blk, blk), 1)
    strict = col < row
    r2 = lax.broadcasted_iota(jnp.int32, (2 * blk, 2 * blk), 0) % blk
    c2 = lax.broadcasted_iota(jnp.int32, (2 * blk, 2 * blk), 1)
    suffix_and_total = ((c2 >= blk) | (r2 > c2)).astype(BF16)

    def tile(q, j, rest, acc, diagonal):
        start = pl.multiple_of(j * blk, blk)
        kb = k_ref[pl.ds(start, blk), :]
        vb = v_ref[pl.ds(start, blk), :]
        z = _qk(q, kb, scale)
        soft = _log1pexp_neg_abs(z)
        log_rest = -(jnp.maximum(z, 0.0) + soft)
        log_sig = jnp.minimum(z, 0.0) - soft
        if diagonal:
            log_rest = jnp.where(strict, log_rest, 0.0)
        hi = log_rest.astype(BF16)
        lo = (log_rest - hi.astype(F32)).astype(BF16)
        sums = jnp.dot(jnp.concatenate([hi, lo], axis=1), suffix_and_total,
                       preferred_element_type=F32)
        w = jnp.exp(log_sig + sums[:, :blk] + rest)
        if diagonal:
            w = jnp.where(strict, w, 0.0)
        acc = acc + jnp.dot(w.astype(BF16), vb, preferred_element_type=F32)
        return rest + sums[:, blk:], acc

    def q_block(qi, _):
        qs = pl.multiple_of(qi * blk, blk)
        q = q_ref[pl.ds(qs, blk), :]
        zeros = jnp.zeros((blk, blk), F32)
        rest, acc = tile(q, qi, zeros, jnp.zeros((blk, HEAD_DIM), F32), True)

        def body(n, carry):
            return tile(q, qi - 1 - n, carry[0], carry[1], False)

        rest, acc = lax.fori_loop(0, qi, body, (rest, acc))
        o_ref[pl.ds(qs, blk), :] = acc.astype(o_ref.dtype)
        return 0

    lax.fori_loop(0, seq // blk, q_block, 0)


def _forget_head(q_ref, k_ref, v_ref, f_ref, o_ref, scale):
    blk = ATTN_BLOCK
    seq = q_ref.shape[0]
    row = lax.broadcasted_iota(jnp.int32, (blk, blk), 0)
    col = lax.broadcasted_iota(jnp.int32, (blk, blk), 1)
    causal = col <= row

    def tile(q, j, m, l, acc, diagonal):
        start = pl.multiple_of(j * blk, blk)
        kb = k_ref[pl.ds(start, blk), :]
        vb = v_ref[pl.ds(start, blk), :]
        z = _qk(q, kb, scale) - f_ref[:, pl.ds(start, blk)]
        if diagonal:
            z = jnp.where(causal, z, -jnp.inf)
        m_new = jnp.maximum(m, jnp.max(z, axis=1, keepdims=True))
        p = jnp.exp(z - m_new)
        alpha = jnp.exp(m - m_new)
        l = alpha * l + jnp.sum(p, axis=1, keepdims=True)
        acc = alpha * acc + jnp.dot(p.astype(BF16), vb, preferred_element_type=F32)
        return m_new, l, acc

    def q_block(qi, _):
        qs = pl.multiple_of(qi * blk, blk)
        q = q_ref[pl.ds(qs, blk), :]
        m, l, acc = tile(q, qi, jnp.full((blk, 1), -jnp.inf, F32), jnp.zeros((blk, 1), F32),
                         jnp.zeros((blk, HEAD_DIM), F32), True)

        def body(n, carry):
            return tile(q, qi - 1 - n, carry[0], carry[1], carry[2], False)

        m, l, acc = lax.fori_loop(0, qi, body, (m, l, acc))
        o_ref[pl.ds(qs, blk), :] = (acc / l).astype(o_ref.dtype)
        return 0

    lax.fori_loop(0, seq // blk, q_block, 0)


def _attn_kernel(q_ref, k_ref, v_ref, f_ref, o_ref, *, n_sb, scale):
    head = pl.program_id(1)

    @pl.when(head < n_sb)
    def _():
        _stickbreak_head(q_ref, k_ref, v_ref, o_ref, scale)

    @pl.when(head >= n_sb)
    def _():
        _forget_head(q_ref, k_ref, v_ref, f_ref, o_ref, scale)


def _attention(qkv, f_rows, batch, seq, n_heads, n_sb):
    d = n_heads * HEAD_DIM
    kern = functools.partial(_attn_kernel, n_sb=n_sb, scale=HEAD_DIM ** -0.5)
    return pl.pallas_call(
        kern,
        grid=(batch, n_heads),
        in_specs=[
            pl.BlockSpec((None, seq, HEAD_DIM), lambda b, h: (b, 0, h)),
            pl.BlockSpec((None, seq, HEAD_DIM), lambda b, h: (b, 0, n_heads + h)),
            pl.BlockSpec((None, seq, HEAD_DIM), lambda b, h: (b, 0, 2 * n_heads + h)),
            pl.BlockSpec((None, None, 1, seq), lambda b, h: (b, jnp.maximum(h - n_sb, 0), 0, 0)),
        ],
        out_specs=pl.BlockSpec((None, seq, HEAD_DIM), lambda b, h: (b, 0, h)),
        out_shape=jax.ShapeDtypeStruct((batch, seq, d), BF16),
        compiler_params=_params("parallel", "parallel"),
        name="attention",
    )(qkv, qkv, qkv, f_rows)


def _residual_and_next(x, y, mod_ref, gate_row, xn_ref, nxt):
    xn = x + mod_ref[gate_row:gate_row + 1, :] * y
    xn_ref[...] = xn
    kind = nxt[0]
    if kind == "mod":
        _, gain_ref, gain_row, nmod_ref, shift_row, h_ref = nxt
        h = (_rms(xn, gain_ref[gain_row:gain_row + 1, :]) * (1.0 + nmod_ref[shift_row + 1:shift_row + 2, :])
             + nmod_ref[shift_row:shift_row + 1, :])
        h_ref[...] = h.astype(h_ref.dtype)
    elif kind == "final":
        _, gain_ref, out_ref = nxt
        out_ref[...] = _rms(xn, gain_ref[...])


def _attn_out_kernel(o_ref, w_ref, x_ref, mod_ref, gain_ref, xn_ref, h_ref):
    y = jnp.dot(o_ref[...], w_ref[...], preferred_element_type=F32)
    _residual_and_next(x_ref[...], y, mod_ref, 2, xn_ref, ("mod", gain_ref, 1, mod_ref, 3, h_ref))


def _attn_out(o2, w_out, x2, mod_l, gain2, seq):
    t, d = x2.shape
    tm = _tile(seq, 512)
    per_seq = seq // tm
    return pl.pallas_call(
        _attn_out_kernel,
        grid=(t // tm,),
        in_specs=[
            pl.BlockSpec((tm, d), lambda i: (i, 0)),
            pl.BlockSpec((d, d), lambda i: (0, 0), pipeline_mode=pl.Buffered(1)),
            pl.BlockSpec((tm, d), lambda i: (i, 0)),
            pl.BlockSpec((None, N_MOD, d), lambda i: (i // per_seq, 0, 0)),
            pl.BlockSpec((2, d), lambda i: (0, 0)),
        ],
        out_specs=[
            pl.BlockSpec((tm, d), lambda i: (i, 0)),
            pl.BlockSpec((tm, d), lambda i: (i, 0)),
        ],
        out_shape=[jax.ShapeDtypeStruct((t, d), F32), jax.ShapeDtypeStruct((t, d), BF16)],
        compiler_params=_params("parallel"),
        name="attn_out",
    )(o2, w_out, x2, mod_l, gain2)


POOL_HALO = 16


def _pool_kernel(x_ref, halo_ref, w_ref, ps_ref, mod_ref, gain_ref, xn_ref, h_ref, hbuf, *, per_seq):
    i = pl.program_id(0)
    tm, d = x_ref.shape
    gw = d // len(POOL_WINDOWS)
    gain1 = gain_ref[0:1, :]
    shift1 = mod_ref[0:1, :]
    scale1 = mod_ref[1:2, :]
    x = x_ref[...]
    hbuf[POOL_HALO:, :] = _rms(x, gain1) * (1.0 + scale1) + shift1
    first = (i % per_seq) == 0

    @pl.when(first)
    def _():
        hbuf[:POOL_HALO, :] = jnp.zeros((POOL_HALO, d), F32)

    @pl.when(jnp.logical_not(first))
    def _():
        hbuf[:POOL_HALO, :] = _rms(halo_ref[...], gain1) * (1.0 + scale1) + shift1

    pos = (i % per_seq) * tm + lax.broadcasted_iota(jnp.int32, (tm, 1), 0)
    ys = []
    for g, win in enumerate(POOL_WINDOWS):
        lanes = slice(g * gw, (g + 1) * gw)
        cur = hbuf[POOL_HALO:, lanes]
        tot = cur
        for back in range(1, win):
            tot = tot + hbuf[POOL_HALO - back:POOL_HALO - back + tm, lanes]
        count = jnp.minimum(pos + 1, win).astype(F32)
        diff = tot / count - cur
        ys.append(jnp.dot(diff.astype(BF16), w_ref[g], preferred_element_type=F32))
    y = jnp.concatenate(ys, axis=1) * ps_ref[...]
    _residual_and_next(x, y, mod_ref, 2, xn_ref, ("mod", gain_ref, 1, mod_ref, 3, h_ref))


def _pool_layer(x2, w_pool, pool_scale, mod_l, gain2, seq):
    t, d = x2.shape
    tm = _tile(seq, 512)
    per_seq = seq // tm
    halo_per_tile = tm // POOL_HALO
    ng, gw, _ = w_pool.shape
    kern = functools.partial(_pool_kernel, per_seq=per_seq)
    return pl.pallas_call(
        kern,
        grid=(t // tm,),
        in_specs=[
            pl.BlockSpec((tm, d), lambda i: (i, 0)),
            pl.BlockSpec((POOL_HALO, d), lambda i: (jnp.maximum(i * halo_per_tile - 1, 0), 0)),
            pl.BlockSpec((ng, gw, gw), lambda i: (0, 0, 0), pipeline_mode=pl.Buffered(1)),
            pl.BlockSpec((1, d), lambda i: (0, 0)),
            pl.BlockSpec((None, N_MOD, d), lambda i: (i // per_seq, 0, 0)),
            pl.BlockSpec((2, d), lambda i: (0, 0)),
        ],
        out_specs=[
            pl.BlockSpec((tm, d), lambda i: (i, 0)),
            pl.BlockSpec((tm, d), lambda i: (i, 0)),
        ],
        out_shape=[jax.ShapeDtypeStruct((t, d), F32), jax.ShapeDtypeStruct((t, d), BF16)],
        scratch_shapes=[pltpu.VMEM((POOL_HALO + tm, d), F32)],
        compiler_params=_params("parallel"),
        name="pool_mixer",
    )(x2, x2, w_pool, pool_scale, mod_l, gain2)


CONV_PAD = V7X_SUBLANES


def _ffn_up_kernel(h_ref, wa_ref, wg_ref, cwa_ref, cwg_ref, cba_ref, cbg_ref, o_ref, sa_ref, sg_ref,
                   *, per_seq):
    i = pl.program_id(1)
    tm = h_ref.shape[0]
    first = (i % per_seq) == 0
    h = h_ref[...]

    def conv_half(w_ref, cw_ref, cb_ref, s_ref):
        @pl.when(first)
        def _():
            s_ref[:CONV_PAD, :] = jnp.zeros((CONV_PAD, s_ref.shape[1]), F32)

        @pl.when(jnp.logical_not(first))
        def _():
            s_ref[:CONV_PAD, :] = s_ref[tm:tm + CONV_PAD, :]

        s_ref[CONV_PAD:, :] = jnp.dot(h, w_ref[...], preferred_element_type=F32)
        y = cb_ref[...]
        for tap in range(CONV_WIDTH):
            back = CONV_WIDTH - 1 - tap
            y = y + s_ref[CONV_PAD - back:CONV_PAD - back + tm, :] * cw_ref[tap:tap + 1, :]
        return y

    a = conv_half(wa_ref, cwa_ref, cba_ref, sa_ref)
    g = conv_half(wg_ref, cwg_ref, cbg_ref, sg_ref)
    o_ref[...] = ((g * _sigmoid(g)) * a).astype(o_ref.dtype)


def _ffn_up(h, w_up, conv_w, conv_b, seq):
    t, d = h.shape
    f2 = w_up.shape[1]
    dff = f2 // 2
    tm = _tile(seq, 1024)
    per_seq = seq // tm
    tn = _tile(dff, 512)
    nj = dff // tn
    kern = functools.partial(_ffn_up_kernel, per_seq=per_seq)
    return pl.pallas_call(
        kern,
        grid=(nj, t // tm),
        in_specs=[
            pl.BlockSpec((tm, d), lambda j, i: (i, 0)),
            pl.BlockSpec((d, tn), lambda j, i: (0, j)),
            pl.BlockSpec((d, tn), lambda j, i: (0, nj + j)),
            pl.BlockSpec((CONV_WIDTH, tn), lambda j, i: (0, j)),
            pl.BlockSpec((CONV_WIDTH, tn), lambda j, i: (0, nj + j)),
            pl.BlockSpec((1, tn), lambda j, i: (0, j)),
            pl.BlockSpec((1, tn), lambda j, i: (0, nj + j)),
        ],
        out_specs=pl.BlockSpec((tm, tn), lambda j, i: (i, j)),
        out_shape=jax.ShapeDtypeStruct((t, dff), BF16),
        scratch_shapes=[pltpu.VMEM((CONV_PAD + tm, tn), F32), pltpu.VMEM((CONV_PAD + tm, tn), F32)],
        compiler_params=_params("parallel", "arbitrary"),
        name="ffn_up",
    )(h, w_up, w_up, conv_w, conv_w, conv_b, conv_b)


def _ffn_down_kernel(*refs, nk, nxt_kind):
    if nxt_kind == "mod":
        a_ref, w_ref, x_ref, mod_ref, gain_ref, nmod_ref, xn_ref, h_ref = refs
        nxt = ("mod", gain_ref, 0, nmod_ref, 0, h_ref)
    elif nxt_kind == "final":
        a_ref, w_ref, x_ref, mod_ref, gain_ref, xn_ref, out_ref = refs
        nxt = ("final", gain_ref, out_ref)
    else:
        a_ref, w_ref, x_ref, mod_ref, xn_ref = refs
        nxt = ("none",)
    k = pl.program_id(1)
    part = jnp.dot(a_ref[...], w_ref[...], preferred_element_type=F32)

    @pl.when(k == 0)
    def _():
        xn_ref[...] = part

    @pl.when(k > 0)
    def _():
        xn_ref[...] += part

    @pl.when(k == nk - 1)
    def _():
        _residual_and_next(x_ref[...], xn_ref[...], mod_ref, 5, xn_ref, nxt)


def _ffn_down(act, w_down, x2, mod_l, seq, nxt_kind, gain=None, nmod=None):
    t, d = x2.shape
    dff = act.shape[1]
    tm = _tile(seq, 512)
    per_seq = seq // tm
    tk = _tile(dff, 512)
    nk = dff // tk
    row = lambda i, k: (i, 0)
    in_specs = [
        pl.BlockSpec((tm, tk), lambda i, k: (i, k)),
        pl.BlockSpec((tk, d), lambda i, k: (k, 0)),
        pl.BlockSpec((tm, d), row),
        pl.BlockSpec((None, N_MOD, d), lambda i, k: (i // per_seq, 0, 0)),
    ]
    args = [act, w_down, x2, mod_l]
    out_specs = [pl.BlockSpec((tm, d), row)]
    out_shape = [jax.ShapeDtypeStruct((t, d), F32)]
    if nxt_kind == "mod":
        in_specs += [pl.BlockSpec((2, d), lambda i, k: (0, 0)),
                     pl.BlockSpec((None, N_MOD, d), lambda i, k: (i // per_seq, 0, 0))]
        args += [gain, nmod]
        out_specs.append(pl.BlockSpec((tm, d), row))
        out_shape.append(jax.ShapeDtypeStruct((t, d), BF16))
    elif nxt_kind == "final":
        in_specs.append(pl.BlockSpec((1, d), lambda i, k: (0, 0)))
        args.append(gain)
        out_specs.append(pl.BlockSpec((tm, d), row))
        out_shape.append(jax.ShapeDtypeStruct((t, d), F32))
    kern = functools.partial(_ffn_down_kernel, nk=nk, nxt_kind=nxt_kind)
    return pl.pallas_call(
        kern,
        grid=(t // tm, nk),
        in_specs=in_specs,
        out_specs=out_specs,
        out_shape=out_shape,
        compiler_params=_params("parallel", "arbitrary"),
        name="ffn_down",
    )(*args)


def kernel(x, c, w_mod, b_mod, norm_gain, w_attn_in, b_forget, w_attn_out, w_pool, pool_scale,
           w_up, conv_w, conv_b, w_down, final_gain):
    batch, seq, d = x.shape
    depth = w_mod.shape[0]
    n_heads = d // HEAD_DIM
    n_fox = b_forget.shape[1]
    n_sb = n_heads - n_fox
    assert seq % ATTN_BLOCK == 0 and d % (len(POOL_WINDOWS) * V7X_LANES) == 0

    mod = _mod_all(c, w_mod, b_mod)
    x2 = x.reshape(batch * seq, d)
    h = _prenorm(x2, norm_gain[0], mod[0], seq)
    out = None
    for l in range(depth):
        i = l // 2
        if l % 2 == 0:
            w_in = w_attn_in[i].astype(BF16)
            qkv = _matmul_cols(h, w_in, 3 * d)
            wf = jnp.pad(w_in[:, 3 * d:], ((0, 0), (0, V7X_LANES - n_fox)))
            bf = jnp.pad(b_forget[i], (0, V7X_LANES - n_fox)).reshape(1, V7X_LANES)
            fcum = _forget_cumsum(h, wf, bf, batch, seq)
            f_rows = fcum.reshape(batch, seq, V7X_LANES)[:, :, :n_fox].transpose(0, 2, 1)
            f_rows = f_rows.reshape(batch, n_fox, 1, seq)
            o = _attention(qkv.reshape(batch, seq, 3 * d), f_rows, batch, seq, n_heads, n_sb)
            x2, h = _attn_out(o.reshape(batch * seq, d), w_attn_out[i].astype(BF16), x2, mod[l],
                              norm_gain[l], seq)
        else:
            x2, h = _pool_layer(x2, w_pool[i].astype(BF16), pool_scale[i].reshape(1, d), mod[l],
                                norm_gain[l], seq)
        act = _ffn_up(h, w_up[l].astype(BF16), conv_w[l], conv_b[l].reshape(1, -1), seq)
        w_dn = w_down[l].astype(BF16)
        if l == depth - 1:
            x2, out = _ffn_down(act, w_dn, x2, mod[l], seq, "final", gain=final_gain.reshape(1, d))
        elif (l + 1) % 2 == 0:
            x2, h = _ffn_down(act, w_dn, x2, mod[l], seq, "mod", gain=norm_gain[l + 1], nmod=mod[l + 1])
        else:
            (x2,) = _ffn_down(act, w_dn, x2, mod[l], seq, "none")
    return out.reshape(batch, seq, d)
```

```python
import functools

import jax
import jax.numpy as jnp
from jax import lax
from jax.experimental import pallas as pl
from jax.experimental.pallas import tpu as pltpu

HEAD_DIM = 128
POOL_WINDOWS = (2, 4, 8, 16)
CONV_WIDTH = 3
N_MOD = 6
EPS = 1e-6

V7X_LANES = 128
V7X_SUBLANES = 8
V7X_VMEM_BYTES = 64 * 1024 * 1024
VMEM_LIMIT_BYTES = V7X_VMEM_BYTES - 8 * 1024 * 1024

F32 = jnp.float32
BF16 = jnp.bfloat16


def _tile(dim, pref):
    t = min(dim, pref)
    while dim % t:
        t //= 2
    return t


def _params(*sem):
    return pltpu.CompilerParams(dimension_semantics=sem, vmem_limit_bytes=VMEM_LIMIT_BYTES)


def _rms(x, gain):
    ms = jnp.mean(x * x, axis=-1, keepdims=True)
    return (x * lax.rsqrt(ms + EPS)) * gain


def _sigmoid(x):
    return 1.0 / (1.0 + jnp.exp(-x))


def _log1pexp_neg_abs(z):
    return jnp.log(1.0 + jnp.exp(-jnp.abs(z)))


def _mod_kernel(c_ref, w_ref, b_ref, o_ref):
    c = c_ref[...]
    cond = c * _sigmoid(c)
    o_ref[...] = jnp.dot(cond.astype(BF16), w_ref[...].astype(BF16),
                         preferred_element_type=F32) + b_ref[...]


def _mod_all(c, w_mod, b_mod):
    depth, d, n = w_mod.shape
    b = c.shape[0]
    rows = -(-b // V7X_SUBLANES) * V7X_SUBLANES
    c_pad = jnp.pad(c, ((0, rows - b), (0, 0)))
    tn = _tile(n, 1024)
    out = pl.pallas_call(
        _mod_kernel,
        grid=(depth, n // tn),
        in_specs=[
            pl.BlockSpec((rows, d), lambda l, j: (0, 0)),
            pl.BlockSpec((None, d, tn), lambda l, j: (l, 0, j)),
            pl.BlockSpec((None, 1, tn), lambda l, j: (l, 0, j)),
        ],
        out_specs=pl.BlockSpec((None, rows, tn), lambda l, j: (l, 0, j)),
        out_shape=jax.ShapeDtypeStruct((depth, rows, n), F32),
        compiler_params=_params("parallel", "parallel"),
        name="adaln_mod",
    )(c_pad, w_mod, b_mod.reshape(depth, 1, n))
    return out[:, :b].reshape(depth, b, N_MOD, d)


def _mod_spec(layer, per_seq, d, ngrid):
    if ngrid == 1:
        return pl.BlockSpec((None, None, N_MOD, d), lambda i: (layer, i // per_seq, 0, 0))
    return pl.BlockSpec((None, None, N_MOD, d), lambda i, j: (layer, i // per_seq, 0, 0))


def _gain_spec(layer, d, ngrid):
    if ngrid == 1:
        return pl.BlockSpec((None, 2, d), lambda i: (layer, 0, 0))
    return pl.BlockSpec((None, 2, d), lambda i, j: (layer, 0, 0))


def _prenorm_kernel(x_ref, gain_ref, mod_ref, h_ref):
    h = _rms(x_ref[...], gain_ref[0:1, :]) * (1.0 + mod_ref[1:2, :]) + mod_ref[0:1, :]
    h_ref[...] = h.astype(h_ref.dtype)


def _prenorm(x2, norm_gain, mod, layer, seq):
    t, d = x2.shape
    tm = _tile(seq, 512)
    per_seq = seq // tm
    return pl.pallas_call(
        _prenorm_kernel,
        grid=(t // tm,),
        in_specs=[
            pl.BlockSpec((tm, d), lambda i: (i, 0)),
            _gain_spec(layer, d, 1),
            _mod_spec(layer, per_seq, d, 1),
        ],
        out_specs=pl.BlockSpec((tm, d), lambda i: (i, 0)),
        out_shape=jax.ShapeDtypeStruct((t, d), BF16),
        compiler_params=_params("parallel"),
        name="prenorm",
    )(x2, norm_gain, mod)


def _qkv_kernel(h_ref, w_ref, o_ref, wb_ref):
    @pl.when(pl.program_id(1) == 0)
    def _():
        wb_ref[...] = w_ref[...].astype(BF16)

    o_ref[...] = jnp.dot(h_ref[...], wb_ref[...], preferred_element_type=F32).astype(o_ref.dtype)


def _qkv_proj(h, w_in, layer, n_out):
    t, k = h.shape
    tm = _tile(t, 1024)
    tn = _tile(n_out, 1024)
    return pl.pallas_call(
        _qkv_kernel,
        grid=(n_out // tn, t // tm),
        in_specs=[
            pl.BlockSpec((tm, k), lambda j, i: (i, 0)),
            pl.BlockSpec((None, k, tn), lambda j, i: (layer, 0, j)),
        ],
        out_specs=pl.BlockSpec((tm, tn), lambda j, i: (i, j)),
        out_shape=jax.ShapeDtypeStruct((t, n_out), BF16),
        scratch_shapes=[pltpu.VMEM((k, tn), BF16)],
        compiler_params=_params("parallel", "arbitrary"),
        name="qkv_proj",
    )(h, w_in)


def _forget_kernel(h_ref, wf_ref, b_ref, o_ref):
    f = jnp.dot(h_ref[...], wf_ref[...], preferred_element_type=F32) + b_ref[...]
    ls = jnp.minimum(f, 0.0) - _log1pexp_neg_abs(f)
    seq = f.shape[0]
    blk = V7X_LANES
    row = lax.broadcasted_iota(jnp.int32, (blk, blk), 0)
    col = lax.broadcasted_iota(jnp.int32, (blk, blk), 1)
    ltri = (col <= row).astype(BF16)
    carry = jnp.zeros((1, ls.shape[1]), F32)
    for i in range(seq // blk):
        x = ls[i * blk:(i + 1) * blk]
        hi = x.astype(BF16)
        r1 = x - hi.astype(F32)
        mid = r1.astype(BF16)
        lo = (r1 - mid.astype(F32)).astype(BF16)
        c = (jnp.dot(ltri, hi, preferred_element_type=F32)
             + jnp.dot(ltri, mid, preferred_element_type=F32)
             + jnp.dot(ltri, lo, preferred_element_type=F32)) + carry
        o_ref[i * blk:(i + 1) * blk, :] = c
        carry = c[blk - 1:blk, :]


def _forget_cumsum(h, wf_pad, bf_pad, batch, seq):
    t, d = h.shape
    return pl.pallas_call(
        _forget_kernel,
        grid=(batch,),
        in_specs=[
            pl.BlockSpec((seq, d), lambda b: (b, 0)),
            pl.BlockSpec((d, V7X_LANES), lambda b: (0, 0)),
            pl.BlockSpec((1, V7X_LANES), lambda b: (0, 0)),
        ],
        out_specs=pl.BlockSpec((seq, V7X_LANES), lambda b: (b, 0)),
        out_shape=jax.ShapeDtypeStruct((t, V7X_LANES), F32),
        compiler_params=_params("parallel"),
        name="forget_cumsum",
    )(h, wf_pad, bf_pad)


ATTN_Q_ROWS = 512
ATTN_K_COLS = 512
ATTN_DIAG_COLS = 256
CUM_COLS = V7X_LANES


def _qk(q, kb, scale):
    return lax.dot_general(q, kb, (((1,), (1,)), ((), ())), preferred_element_type=F32) * scale


def _diag_tiles():
    return [(c0, ATTN_Q_ROWS - c0, c0, ATTN_DIAG_COLS) for c0 in range(0, ATTN_Q_ROWS, ATTN_DIAG_COLS)]


def _stickbreak_head(q_ref, k_ref, v_ref, o_ref, acc_ref, rest_ref, scale):
    seq = q_ref.shape[0]
    sub = CUM_COLS
    r2 = lax.broadcasted_iota(jnp.int32, (2 * sub, 2 * sub), 0) % sub
    c2 = lax.broadcasted_iota(jnp.int32, (2 * sub, 2 * sub), 1)
    suffix_and_total = ((c2 >= sub) | (r2 > c2)).astype(BF16)

    def tile(qs, row0, nrows, ks, ncols, diagonal):
        rows = pl.ds(row0, nrows)
        ks = pl.multiple_of(ks, CUM_COLS)
        q = q_ref[pl.ds(pl.multiple_of(qs + row0, CUM_COLS), nrows), :]
        kb = k_ref[pl.ds(ks, ncols), :]
        vb = v_ref[pl.ds(ks, ncols), :]
        z = _qk(q, kb, scale)
        soft = _log1pexp_neg_abs(z)
        log_rest = -(jnp.maximum(z, 0.0) + soft)
        log_sig = jnp.minimum(z, 0.0) - soft
        if diagonal:
            strict = (lax.broadcasted_iota(jnp.int32, (nrows, ncols), 1)
                      < lax.broadcasted_iota(jnp.int32, (nrows, ncols), 0))
            log_rest = jnp.where(strict, log_rest, 0.0)
        run = rest_ref[rows, :]
        log_after = [None] * (ncols // sub)
        for s in reversed(range(ncols // sub)):
            lr = log_rest[:, s * sub:(s + 1) * sub]
            hi = lr.astype(BF16)
            lo = (lr - hi.astype(F32)).astype(BF16)
            sums = jnp.dot(jnp.concatenate([hi, lo], axis=1), suffix_and_total,
                           preferred_element_type=F32)
            log_after[s] = sums[:, :sub] + run
            run = run + sums[:, sub:]
        w = jnp.exp(log_sig + jnp.concatenate(log_after, axis=1))
        if diagonal:
            w = jnp.where(strict, w, 0.0)
        acc_ref[rows, :] += jnp.dot(w.astype(BF16), vb, preferred_element_type=F32)
        rest_ref[rows, :] = run

    def q_block(qi, _):
        qs = pl.multiple_of(qi * ATTN_Q_ROWS, ATTN_Q_ROWS)
        acc_ref[...] = jnp.zeros_like(acc_ref)
        rest_ref[...] = jnp.zeros_like(rest_ref)
        for row0, nrows, col0, ncols in reversed(_diag_tiles()):
            tile(qs, row0, nrows, qs + col0, ncols, True)

        def body(n, _):
            ks = pl.multiple_of((qi - 1 - n) * ATTN_K_COLS, ATTN_K_COLS)
            tile(qs, 0, ATTN_Q_ROWS, ks, ATTN_K_COLS, False)
            return 0

        lax.fori_loop(0, qi * (ATTN_Q_ROWS // ATTN_K_COLS), body, 0)
        o_ref[pl.ds(qs, ATTN_Q_ROWS), :] = acc_ref[...].astype(o_ref.dtype)
        return 0

    lax.fori_loop(0, seq // ATTN_Q_ROWS, q_block, 0)


def _forget_head(q_ref, k_ref, v_ref, f_ref, o_ref, acc_ref, m_ref, l_ref, scale):
    seq = q_ref.shape[0]

    def tile(qs, row0, nrows, ks, ncols, diagonal):
        rows = pl.ds(row0, nrows)
        ks = pl.multiple_of(ks, CUM_COLS)
        q = q_ref[pl.ds(pl.multiple_of(qs + row0, CUM_COLS), nrows), :]
        kb = k_ref[pl.ds(ks, ncols), :]
        vb = v_ref[pl.ds(ks, ncols), :]
        z = _qk(q, kb, scale) - f_ref[:, pl.ds(ks, ncols)]
        if diagonal:
            causal = (lax.broadcasted_iota(jnp.int32, (nrows, ncols), 1)
                      <= lax.broadcasted_iota(jnp.int32, (nrows, ncols), 0))
            z = jnp.where(causal, z, -jnp.inf)
        m_old = m_ref[rows, :]
        m_new = jnp.maximum(m_old, jnp.max(z, axis=1, keepdims=True))
        p = jnp.exp(z - m_new)
        alpha = jnp.exp(m_old - m_new)
        l_ref[rows, :] = alpha * l_ref[rows, :] + jnp.sum(p, axis=1, keepdims=True)
        acc_ref[rows, :] = alpha * acc_ref[rows, :] + jnp.dot(p.astype(BF16), vb,
                                                            preferred_element_type=F32)
        m_ref[rows, :] = m_new

    def q_block(qi, _):
        qs = pl.multiple_of(qi * ATTN_Q_ROWS, ATTN_Q_ROWS)
        acc_ref[...] = jnp.zeros_like(acc_ref)
        l_ref[...] = jnp.zeros_like(l_ref)
        m_ref[...] = jnp.full(m_ref.shape, -jnp.inf, F32)
        for row0, nrows, col0, ncols in _diag_tiles():
            tile(qs, row0, nrows, qs + col0, ncols, True)

        def body(n, _):
            ks = pl.multiple_of(n * ATTN_K_COLS, ATTN_K_COLS)
            tile(qs, 0, ATTN_Q_ROWS, ks, ATTN_K_COLS, False)
            return 0

        lax.fori_loop(0, qi * (ATTN_Q_ROWS // ATTN_K_COLS), body, 0)
        o_ref[pl.ds(qs, ATTN_Q_ROWS), :] = (acc_ref[...] / l_ref[...]).astype(o_ref.dtype)
        return 0

    lax.fori_loop(0, seq // ATTN_Q_ROWS, q_block, 0)


def _attn_kernel(q_ref, k_ref, v_ref, f_ref, o_ref, acc_ref, rest_ref, m_ref, l_ref, *, n_sb, scale):
    head = pl.program_id(1)

    @pl.when(head < n_sb)
    def _():
        _stickbreak_head(q_ref, k_ref, v_ref, o_ref, acc_ref, rest_ref, scale)

    @pl.when(head >= n_sb)
    def _():
        _forget_head(q_ref, k_ref, v_ref, f_ref, o_ref, acc_ref, m_ref, l_ref, scale)


def _attention(qkv, f_rows, batch, seq, n_heads, n_sb):
    d = n_heads * HEAD_DIM
    kern = functools.partial(_attn_kernel, n_sb=n_sb, scale=HEAD_DIM ** -0.5)
    return pl.pallas_call(
        kern,
        grid=(batch, n_heads),
        in_specs=[
            pl.BlockSpec((None, seq, HEAD_DIM), lambda b, h: (b, 0, h)),
            pl.BlockSpec((None, seq, HEAD_DIM), lambda b, h: (b, 0, n_heads + h)),
            pl.BlockSpec((None, seq, HEAD_DIM), lambda b, h: (b, 0, 2 * n_heads + h)),
            pl.BlockSpec((None, None, 1, seq), lambda b, h: (b, jnp.maximum(h - n_sb, 0), 0, 0)),
        ],
        out_specs=pl.BlockSpec((None, seq, HEAD_DIM), lambda b, h: (b, 0, h)),
        out_shape=jax.ShapeDtypeStruct((batch, seq, d), BF16),
        scratch_shapes=[
            pltpu.VMEM((ATTN_Q_ROWS, HEAD_DIM), F32),
            pltpu.VMEM((ATTN_Q_ROWS, CUM_COLS), F32),
            pltpu.VMEM((ATTN_Q_ROWS, 1), F32),
            pltpu.VMEM((ATTN_Q_ROWS, 1), F32),
        ],
        compiler_params=_params("parallel", "parallel"),
        name="attention",
    )(qkv, qkv, qkv, f_rows)


def _next_stage(xn, nxt):
    kind = nxt[0]
    if kind == "mod":
        _, gain_ref, gain_row, nmod_ref, shift_row, h_ref = nxt
        h = (_rms(xn, gain_ref[gain_row:gain_row + 1, :]) * (1.0 + nmod_ref[shift_row + 1:shift_row + 2, :])
             + nmod_ref[shift_row:shift_row + 1, :])
        h_ref[...] = h.astype(h_ref.dtype)
    elif kind == "final":
        _, gain_ref, out_ref = nxt
        out_ref[...] = _rms(xn, gain_ref[...])


def _attn_out_kernel(o_ref, w_ref, x_ref, mod_ref, gain_ref, xn_ref, h_ref):
    y = jnp.dot(o_ref[...], w_ref[...], preferred_element_type=F32)
    xn = x_ref[...] + mod_ref[2:3, :] * y
    xn_ref[...] = xn
    _next_stage(xn, ("mod", gain_ref, 1, mod_ref, 3, h_ref))


def _attn_out(o2, w_out, wl, x2, mod, norm_gain, layer, seq):
    t, d = x2.shape
    tm = _tile(seq, 512)
    per_seq = seq // tm
    return pl.pallas_call(
        _attn_out_kernel,
        grid=(t // tm,),
        in_specs=[
            pl.BlockSpec((tm, d), lambda i: (i, 0)),
            pl.BlockSpec((None, d, d), lambda i: (wl, 0, 0), pipeline_mode=pl.Buffered(1)),
            pl.BlockSpec((tm, d), lambda i: (i, 0)),
            _mod_spec(layer, per_seq, d, 1),
            _gain_spec(layer, d, 1),
        ],
        out_specs=[
            pl.BlockSpec((tm, d), lambda i: (i, 0)),
            pl.BlockSpec((tm, d), lambda i: (i, 0)),
        ],
        out_shape=[jax.ShapeDtypeStruct((t, d), F32), jax.ShapeDtypeStruct((t, d), BF16)],
        compiler_params=_params("parallel"),
        name="attn_out",
    )(o2, w_out, x2, mod, norm_gain)


POOL_HALO = 16


def _pool_kernel(x_ref, halo_ref, w_ref, ps_ref, mod_ref, gain_ref, xn_ref, h_ref, hbuf, *, per_seq):
    i = pl.program_id(0)
    tm, d = x_ref.shape
    gw = d // len(POOL_WINDOWS)
    gain1 = gain_ref[0:1, :]
    shift1 = mod_ref[0:1, :]
    scale1 = mod_ref[1:2, :]
    x = x_ref[...]
    hbuf[POOL_HALO:, :] = _rms(x, gain1) * (1.0 + scale1) + shift1
    first = (i % per_seq) == 0

    @pl.when(first)
    def _():
        hbuf[:POOL_HALO, :] = jnp.zeros((POOL_HALO, d), F32)

    @pl.when(jnp.logical_not(first))
    def _():
        hbuf[:POOL_HALO, :] = _rms(halo_ref[...], gain1) * (1.0 + scale1) + shift1

    pos = (i % per_seq) * tm + lax.broadcasted_iota(jnp.int32, (tm, 1), 0)
    ys = []
    for g, win in enumerate(POOL_WINDOWS):
        lanes = slice(g * gw, (g + 1) * gw)
        cur = hbuf[POOL_HALO:, lanes]
        tot = cur
        for back in range(1, win):
            tot = tot + hbuf[POOL_HALO - back:POOL_HALO - back + tm, lanes]
        count = jnp.minimum(pos + 1, win).astype(F32)
        diff = tot / count - cur
        ys.append(jnp.dot(diff.astype(BF16), w_ref[g], preferred_element_type=F32))
    y = jnp.concatenate(ys, axis=1) * ps_ref[...]
    xn = x + mod_ref[2:3, :] * y
    xn_ref[...] = xn
    _next_stage(xn, ("mod", gain_ref, 1, mod_ref, 3, h_ref))


def _pool_layer(x2, w_pool, pool_scale, wl, mod, norm_gain, layer, seq):
    t, d = x2.shape
    tm = _tile(seq, 512)
    per_seq = seq // tm
    halo_per_tile = tm // POOL_HALO
    _, ng, gw, _ = w_pool.shape
    kern = functools.partial(_pool_kernel, per_seq=per_seq)
    return pl.pallas_call(
        kern,
        grid=(t // tm,),
        in_specs=[
            pl.BlockSpec((tm, d), lambda i: (i, 0)),
            pl.BlockSpec((POOL_HALO, d), lambda i: (jnp.maximum(i * halo_per_tile - 1, 0), 0)),
            pl.BlockSpec((None, ng, gw, gw), lambda i: (wl, 0, 0, 0), pipeline_mode=pl.Buffered(1)),
            pl.BlockSpec((None, 1, d), lambda i: (wl, 0, 0)),
            _mod_spec(layer, per_seq, d, 1),
            _gain_spec(layer, d, 1),
        ],
        out_specs=[
            pl.BlockSpec((tm, d), lambda i: (i, 0)),
            pl.BlockSpec((tm, d), lambda i: (i, 0)),
        ],
        out_shape=[jax.ShapeDtypeStruct((t, d), F32), jax.ShapeDtypeStruct((t, d), BF16)],
        scratch_shapes=[pltpu.VMEM((POOL_HALO + tm, d), F32)],
        compiler_params=_params("parallel"),
        name="pool_mixer",
    )(x2, x2, w_pool, pool_scale, mod, norm_gain)


CONV_PAD = V7X_SUBLANES
FFN_UP_CHUNK = 256


def _ffn_up_kernel(h_ref, wa_ref, wg_ref, cwa_ref, cwg_ref, cba_ref, cbg_ref, o_ref,
                   wab_ref, wgb_ref, sa_ref, sg_ref, *, per_seq):
    i = pl.program_id(1)
    tm = h_ref.shape[0]
    chunk = min(FFN_UP_CHUNK, tm)

    @pl.when(i == 0)
    def _():
        wab_ref[...] = wa_ref[...].astype(BF16)
        wgb_ref[...] = wg_ref[...].astype(BF16)

    first = (i % per_seq) == 0

    @pl.when(first)
    def _():
        sa_ref[:CONV_PAD, :] = jnp.zeros((CONV_PAD, sa_ref.shape[1]), F32)
        sg_ref[:CONV_PAD, :] = jnp.zeros((CONV_PAD, sg_ref.shape[1]), F32)

    @pl.when(jnp.logical_not(first))
    def _():
        sa_ref[:CONV_PAD, :] = sa_ref[tm:tm + CONV_PAD, :]
        sg_ref[:CONV_PAD, :] = sg_ref[tm:tm + CONV_PAD, :]

    def conv(s_ref, cw_ref, cb_ref, r0):
        y = cb_ref[...]
        for tap in range(CONV_WIDTH):
            back = CONV_WIDTH - 1 - tap
            y = y + s_ref[CONV_PAD + r0 - back:CONV_PAD + r0 - back + chunk, :] * cw_ref[tap:tap + 1, :]
        return y

    for r0 in range(0, tm, chunk):
        hc = h_ref[r0:r0 + chunk, :]
        sa_ref[CONV_PAD + r0:CONV_PAD + r0 + chunk, :] = jnp.dot(hc, wab_ref[...],
                                                                  preferred_element_type=F32)
        sg_ref[CONV_PAD + r0:CONV_PAD + r0 + chunk, :] = jnp.dot(hc, wgb_ref[...],
                                                                  preferred_element_type=F32)
        a = conv(sa_ref, cwa_ref, cba_ref, r0)
        g = conv(sg_ref, cwg_ref, cbg_ref, r0)
        o_ref[r0:r0 + chunk, :] = ((g * _sigmoid(g)) * a).astype(o_ref.dtype)


def _ffn_up(h, w_up, conv_w, conv_b, layer, seq):
    t, d = h.shape
    f2 = w_up.shape[2]
    dff = f2 // 2
    tm = _tile(seq, 1024)
    per_seq = seq // tm
    tn = _tile(dff, 512)
    nj = dff // tn
    kern = functools.partial(_ffn_up_kernel, per_seq=per_seq)
    return pl.pallas_call(
        kern,
        grid=(nj, t // tm),
        in_specs=[
            pl.BlockSpec((tm, d), lambda j, i: (i, 0)),
            pl.BlockSpec((None, d, tn), lambda j, i: (layer, 0, j)),
            pl.BlockSpec((None, d, tn), lambda j, i: (layer, 0, nj + j)),
            pl.BlockSpec((None, CONV_WIDTH, tn), lambda j, i: (layer, 0, j)),
            pl.BlockSpec((None, CONV_WIDTH, tn), lambda j, i: (layer, 0, nj + j)),
            pl.BlockSpec((None, 1, tn), lambda j, i: (layer, 0, j)),
            pl.BlockSpec((None, 1, tn), lambda j, i: (layer, 0, nj + j)),
        ],
        out_specs=pl.BlockSpec((tm, tn), lambda j, i: (i, j)),
        out_shape=jax.ShapeDtypeStruct((t, dff), BF16),
        scratch_shapes=[pltpu.VMEM((d, tn), BF16), pltpu.VMEM((d, tn), BF16),
                        pltpu.VMEM((CONV_PAD + tm, tn), F32), pltpu.VMEM((CONV_PAD + tm, tn), F32)],
        compiler_params=_params("parallel", "arbitrary"),
        name="ffn_up",
    )(h, w_up, w_up, conv_w, conv_w, conv_b, conv_b)


def _ffn_down_kernel(*refs, nn, tn, nxt_kind):
    if nxt_kind == "mod":
        a_ref, w_ref, x_ref, mod_ref, gain_ref, nmod_ref, xn_ref, h_ref = refs
        nxt = ("mod", gain_ref, 0, nmod_ref, 0, h_ref)
    elif nxt_kind == "final":
        a_ref, w_ref, x_ref, mod_ref, gain_ref, xn_ref, out_ref = refs
        nxt = ("final", gain_ref, out_ref)
    else:
        a_ref, w_ref, x_ref, mod_ref, xn_ref = refs
        nxt = ("none",)
    n = pl.program_id(1)
    cols = pl.ds(pl.multiple_of(n * tn, tn), tn)
    y = jnp.dot(a_ref[...], w_ref[...], preferred_element_type=F32)
    xn_ref[:, cols] = x_ref[:, cols] + mod_ref[5:6, cols] * y

    if nxt_kind != "none":
        @pl.when(n == nn - 1)
        def _():
            _next_stage(xn_ref[...], nxt)


def _ffn_down(act, w_down, x2, mod, layer, seq, nxt_kind, gain=None):
    t, d = x2.shape
    dff = act.shape[1]
    tm = _tile(seq, 512)
    per_seq = seq // tm
    tn = _tile(d, 512)
    nn = d // tn
    row = lambda i, n: (i, 0)
    in_specs = [
        pl.BlockSpec((tm, dff), row),
        pl.BlockSpec((None, dff, tn), lambda i, n: (layer, 0, n)),
        pl.BlockSpec((tm, d), row),
        _mod_spec(layer, per_seq, d, 2),
    ]
    args = [act, w_down, x2, mod]
    out_specs = [pl.BlockSpec((tm, d), row)]
    out_shape = [jax.ShapeDtypeStruct((t, d), F32)]
    if nxt_kind == "mod":
        in_specs += [_gain_spec(layer + 1, d, 2), _mod_spec(layer + 1, per_seq, d, 2)]
        args += [gain, mod]
        out_specs.append(pl.BlockSpec((tm, d), row))
        out_shape.append(jax.ShapeDtypeStruct((t, d), BF16))
    elif nxt_kind == "final":
        in_specs.append(pl.BlockSpec((1, d), lambda i, n: (0, 0)))
        args.append(gain)
        out_specs.append(pl.BlockSpec((tm, d), row))
        out_shape.append(jax.ShapeDtypeStruct((t, d), F32))
    kern = functools.partial(_ffn_down_kernel, nn=nn, tn=tn, nxt_kind=nxt_kind)
    return pl.pallas_call(
        kern,
        grid=(t // tm, nn),
        in_specs=in_specs,
        out_specs=out_specs,
        out_shape=out_shape,
        compiler_params=_params("parallel", "arbitrary"),
        name="ffn_down",
    )(*args)


def kernel(x, c, w_mod, b_mod, norm_gain, w_attn_in, b_forget, w_attn_out, w_pool, pool_scale,
           w_up, conv_w, conv_b, w_down, final_gain):
    batch, seq, d = x.shape
    depth = w_mod.shape[0]
    n_heads = d // HEAD_DIM
    n_fox = b_forget.shape[1]
    n_sb = n_heads - n_fox
    assert seq % ATTN_Q_ROWS == 0 and d % (len(POOL_WINDOWS) * V7X_LANES) == 0

    mod = _mod_all(c, w_mod, b_mod)
    w_out_b = w_attn_out.astype(BF16)
    w_pool_b = w_pool.astype(BF16)
    w_down_b = w_down.astype(BF16)
    pool_scale3 = pool_scale.reshape(pool_scale.shape[0], 1, d)
    conv_b3 = conv_b.reshape(depth, 1, conv_b.shape[1])

    x2 = x.reshape(batch * seq, d)
    h = _prenorm(x2, norm_gain, mod, 0, seq)
    out = None
    for l in range(depth):
        i = l // 2
        if l % 2 == 0:
            qkv = _qkv_proj(h, w_attn_in, i, 3 * d)
            wf = jnp.pad(w_attn_in[i, :, 3 * d:].astype(BF16), ((0, 0), (0, V7X_LANES - n_fox)))
            bf = jnp.pad(b_forget[i], (0, V7X_LANES - n_fox)).reshape(1, V7X_LANES)
            fcum = _forget_cumsum(h, wf, bf, batch, seq)
            f_rows = fcum.reshape(batch, seq, V7X_LANES)[:, :, :n_fox].transpose(0, 2, 1)
            f_rows = f_rows.reshape(batch, n_fox, 1, seq)
            o = _attention(qkv.reshape(batch, seq, 3 * d), f_rows, batch, seq, n_heads, n_sb)
            x2, h = _attn_out(o.reshape(batch * seq, d), w_out_b, i, x2, mod, norm_gain, l, seq)
        else:
            x2, h = _pool_layer(x2, w_pool_b, pool_scale3, i, mod, norm_gain, l, seq)
        act = _ffn_up(h, w_up, conv_w, conv_b3, l, seq)
        if l == depth - 1:
            x2, out = _ffn_down(act, w_down_b, x2, mod, l, seq, "final", gain=final_gain.reshape(1, d))
        elif (l + 1) % 2 == 0:
            x2, h = _ffn_down(act, w_down_b, x2, mod, l, seq, "mod", gain=norm_gain)
        else:
            (x2,) = _ffn_down(act, w_down_b, x2, mod, l, seq, "none")
    return out.reshape(batch, seq, d)
```

```python
import functools

import jax
import jax.numpy as jnp
from jax import lax
from jax.experimental import pallas as pl
from jax.experimental.pallas import tpu as pltpu

HEAD_DIM = 128
POOL_WINDOWS = (2, 4, 8, 16)
CONV_WIDTH = 3
N_MOD = 6
EPS = 1e-6

V7X_LANES = 128
V7X_SUBLANES = 8
V7X_VMEM_BYTES = 64 * 1024 * 1024
VMEM_LIMIT_BYTES = V7X_VMEM_BYTES - 8 * 1024 * 1024

F32 = jnp.float32
BF16 = jnp.bfloat16


def _tile(dim, pref):
    t = min(dim, pref)
    while dim % t:
        t //= 2
    return t


def _params(*sem, flags=None):
    return pltpu.CompilerParams(dimension_semantics=sem, vmem_limit_bytes=VMEM_LIMIT_BYTES, flags=flags)


def _rms(x, gain):
    ms = jnp.mean(x * x, axis=-1, keepdims=True)
    return (x * lax.rsqrt(ms + EPS)) * gain


LOG2E = 1.4426950408889634


def _sigmoid(x):
    return 1.0 / (1.0 + jnp.exp2(x * (-LOG2E)))


def _log1pexp_neg_abs(z):
    return jnp.log(1.0 + jnp.exp(-jnp.abs(z)))


def _mod_kernel(c_ref, w_ref, b_ref, o_ref):
    c = c_ref[...]
    cond = c * _sigmoid(c)
    o_ref[...] = jnp.dot(cond.astype(BF16), w_ref[...].astype(BF16),
                         preferred_element_type=F32) + b_ref[...]


def _mod_all(c, w_mod, b_mod):
    depth, d, n = w_mod.shape
    b = c.shape[0]
    rows = -(-b // V7X_SUBLANES) * V7X_SUBLANES
    c_pad = jnp.pad(c, ((0, rows - b), (0, 0)))
    tn = _tile(n, 1024)
    out = pl.pallas_call(
        _mod_kernel,
        grid=(depth, n // tn),
        in_specs=[
            pl.BlockSpec((rows, d), lambda l, j: (0, 0)),
            pl.BlockSpec((None, d, tn), lambda l, j: (l, 0, j)),
            pl.BlockSpec((None, 1, tn), lambda l, j: (l, 0, j)),
        ],
        out_specs=pl.BlockSpec((None, rows, tn), lambda l, j: (l, 0, j)),
        out_shape=jax.ShapeDtypeStruct((depth, rows, n), F32),
        compiler_params=_params("parallel", "parallel"),
        name="adaln_mod",
    )(c_pad, w_mod, b_mod.reshape(depth, 1, n))
    return out[:, :b].reshape(depth, b, N_MOD, d)


def _mod_spec(layer, per_seq, d, ngrid):
    if ngrid == 1:
        return pl.BlockSpec((None, None, N_MOD, d), lambda i: (layer, i // per_seq, 0, 0))
    return pl.BlockSpec((None, None, N_MOD, d), lambda i, j: (layer, i // per_seq, 0, 0))


def _gain_spec(layer, d, ngrid):
    if ngrid == 1:
        return pl.BlockSpec((None, 2, d), lambda i: (layer, 0, 0))
    return pl.BlockSpec((None, 2, d), lambda i, j: (layer, 0, 0))


def _prenorm_kernel(x_ref, gain_ref, mod_ref, h_ref):
    h = _rms(x_ref[...], gain_ref[0:1, :]) * (1.0 + mod_ref[1:2, :]) + mod_ref[0:1, :]
    h_ref[...] = h.astype(h_ref.dtype)


def _prenorm(x2, norm_gain, mod, layer, seq):
    t, d = x2.shape
    tm = _tile(seq, 512)
    per_seq = seq // tm
    return pl.pallas_call(
        _prenorm_kernel,
        grid=(t // tm,),
        in_specs=[
            pl.BlockSpec((tm, d), lambda i: (i, 0)),
            _gain_spec(layer, d, 1),
            _mod_spec(layer, per_seq, d, 1),
        ],
        out_specs=pl.BlockSpec((tm, d), lambda i: (i, 0)),
        out_shape=jax.ShapeDtypeStruct((t, d), BF16),
        compiler_params=_params("parallel"),
        name="prenorm",
    )(x2, norm_gain, mod)


def _qkv_kernel(h_ref, w_ref, o_ref, wb_ref, *, q_tiles, q_factor):
    @pl.when(pl.program_id(1) == 0)
    def _():
        factor = jnp.where(pl.program_id(0) < q_tiles, q_factor, 1.0)
        wb_ref[...] = (w_ref[...] * factor).astype(BF16)

    o_ref[...] = lax.dot_general(h_ref[...], wb_ref[...], (((1,), (1,)), ((), ())),
                                 preferred_element_type=F32).astype(o_ref.dtype)


def _qkv_proj(h, w_in_t, layer, n_out, d, q_factor):
    t, k = h.shape
    tm = _tile(t, 1024)
    tn = _tile(d, 1024)
    kern = functools.partial(_qkv_kernel, q_tiles=d // tn, q_factor=q_factor)
    return pl.pallas_call(
        kern,
        grid=(n_out // tn, t // tm),
        in_specs=[
            pl.BlockSpec((tm, k), lambda j, i: (i, 0)),
            pl.BlockSpec((None, tn, k), lambda j, i: (layer, j, 0)),
        ],
        out_specs=pl.BlockSpec((tm, tn), lambda j, i: (i, j)),
        out_shape=jax.ShapeDtypeStruct((t, n_out), BF16),
        scratch_shapes=[pltpu.VMEM((tn, k), BF16)],
        compiler_params=_params("parallel", "arbitrary"),
        name="qkv_proj",
    )(h, w_in_t)


def _forget_kernel(h_ref, wf_ref, b_ref, o_ref):
    f = lax.dot_general(h_ref[...], wf_ref[...].astype(BF16), (((1,), (1,)), ((), ())),
                        preferred_element_type=F32) + b_ref[...]
    ls = jnp.minimum(f, 0.0) - _log1pexp_neg_abs(f)
    seq = f.shape[0]
    blk = V7X_LANES
    row = lax.broadcasted_iota(jnp.int32, (blk, blk), 0)
    col = lax.broadcasted_iota(jnp.int32, (blk, blk), 1)
    ltri = (col <= row).astype(BF16)
    carry = jnp.zeros((1, ls.shape[1]), F32)
    for i in range(seq // blk):
        x = ls[i * blk:(i + 1) * blk]
        hi = x.astype(BF16)
        r1 = x - hi.astype(F32)
        mid = r1.astype(BF16)
        lo = (r1 - mid.astype(F32)).astype(BF16)
        c = (jnp.dot(ltri, hi, preferred_element_type=F32)
             + jnp.dot(ltri, mid, preferred_element_type=F32)
             + jnp.dot(ltri, lo, preferred_element_type=F32)) + carry
        o_ref[i * blk:(i + 1) * blk, :] = c
        carry = c[blk - 1:blk, :]


def _forget_cumsum(h, wf_pad, bf_pad, batch, seq):
    t, d = h.shape
    return pl.pallas_call(
        _forget_kernel,
        grid=(batch,),
        in_specs=[
            pl.BlockSpec((seq, d), lambda b: (b, 0)),
            pl.BlockSpec((V7X_LANES, d), lambda b: (0, 0)),
            pl.BlockSpec((1, V7X_LANES), lambda b: (0, 0)),
        ],
        out_specs=pl.BlockSpec((seq, V7X_LANES), lambda b: (b, 0)),
        out_shape=jax.ShapeDtypeStruct((t, V7X_LANES), F32),
        compiler_params=_params("parallel"),
        name="forget_cumsum",
    )(h, wf_pad, bf_pad)


ATTN_Q_ROWS = 512
ATTN_K_COLS = 512
ATTN_DIAG_COLS = 256
CUM_COLS = V7X_LANES


ATTN_ROW_SPLIT = 2


def _unrolled(n, body):
    for i in range(n):
        body(i, 0)


def _aligned(x, m):
    return x if isinstance(x, int) else pl.multiple_of(x, m)


def _qk(q, kb):
    return lax.dot_general(q, kb, (((1,), (1,)), ((), ())), preferred_element_type=F32)


def _lane_group_reduce(x, op):
    groups = [x[:, c:c + V7X_LANES] for c in range(0, x.shape[1], V7X_LANES)]
    while len(groups) > 1:
        groups = [op(groups[i], groups[i + 1]) for i in range(0, len(groups) - 1, 2)] + (
            [groups[-1]] if len(groups) % 2 else [])
    return groups[0]


def _diag_tiles():
    return [(c0, ATTN_Q_ROWS - c0, c0, ATTN_DIAG_COLS) for c0 in range(0, ATTN_Q_ROWS, ATTN_DIAG_COLS)]


def _stickbreak_head(q_ref, k_ref, v_ref, o_ref, acc_ref, rest_ref):
    seq = q_ref.shape[0]
    sub = CUM_COLS
    r2 = lax.broadcasted_iota(jnp.int32, (2 * sub, 2 * sub), 0) % sub
    c2 = lax.broadcasted_iota(jnp.int32, (2 * sub, 2 * sub), 1)
    suffix_and_total = ((c2 >= sub) | (r2 > c2)).astype(BF16)

    def tile(qs, row0, nrows, ks, ncols, diagonal):
        rows = pl.ds(row0, nrows)
        ks = _aligned(ks, CUM_COLS)
        q = q_ref[pl.ds(_aligned(qs + row0, CUM_COLS), nrows), :]
        kb = k_ref[pl.ds(ks, ncols), :]
        vb = v_ref[pl.ds(ks, ncols), :]
        u = _qk(q, kb)
        r = jnp.maximum(u, 0.0) + jnp.log(1.0 + jnp.exp2(-jnp.abs(u))) * LOG2E
        if diagonal:
            strict = (lax.broadcasted_iota(jnp.int32, (nrows, ncols), 1)
                      < lax.broadcasted_iota(jnp.int32, (nrows, ncols), 0))
            r = jnp.where(strict, r, 0.0)
        run = rest_ref[rows, :]
        after = [None] * (ncols // sub)
        for s in reversed(range(ncols // sub)):
            rs = r[:, s * sub:(s + 1) * sub]
            hi = rs.astype(BF16)
            lo = (rs - hi.astype(F32)).astype(BF16)
            sums = jnp.dot(jnp.concatenate([hi, lo], axis=1), suffix_and_total,
                           preferred_element_type=F32)
            after[s] = sums[:, :sub] + run
            run = run + sums[:, sub:]
        w = jnp.exp2((u - r) - jnp.concatenate(after, axis=1))
        if diagonal:
            w = jnp.where(strict, w, 0.0)
        acc_ref[rows, :] += jnp.dot(w.astype(BF16), vb, preferred_element_type=F32)
        rest_ref[rows, :] = run

    def q_block(qi, _):
        qs = _aligned(qi * ATTN_Q_ROWS, ATTN_Q_ROWS)
        acc_ref[...] = jnp.zeros_like(acc_ref)
        rest_ref[...] = jnp.zeros_like(rest_ref)
        for row0, nrows, col0, ncols in reversed(_diag_tiles()):
            tile(qs, row0, nrows, qs + col0, ncols, True)

        def body(n, _):
            ks = _aligned((qi - 1 - n) * ATTN_K_COLS, ATTN_K_COLS)
            for row0 in range(0, ATTN_Q_ROWS, ATTN_Q_ROWS // ATTN_ROW_SPLIT):
                tile(qs, row0, ATTN_Q_ROWS // ATTN_ROW_SPLIT, ks, ATTN_K_COLS, False)
            return 0

        _unrolled(qi * (ATTN_Q_ROWS // ATTN_K_COLS), body)
        o_ref[pl.ds(qs, ATTN_Q_ROWS), :] = acc_ref[...].astype(o_ref.dtype)
        return 0

    _unrolled(seq // ATTN_Q_ROWS, q_block)


def _forget_head(q_ref, k_ref, v_ref, f_ref, o_ref, acc_ref, m_ref, l_ref):
    seq = q_ref.shape[0]

    def tile(qs, row0, nrows, ks, ncols, diagonal):
        rows = pl.ds(row0, nrows)
        ks = _aligned(ks, CUM_COLS)
        q = q_ref[pl.ds(_aligned(qs + row0, CUM_COLS), nrows), :]
        kb = k_ref[pl.ds(ks, ncols), :]
        vb = v_ref[pl.ds(ks, ncols), :]
        z = _qk(q, kb) - f_ref[:, pl.ds(ks, ncols)] * LOG2E
        if diagonal:
            causal = (lax.broadcasted_iota(jnp.int32, (nrows, ncols), 1)
                      <= lax.broadcasted_iota(jnp.int32, (nrows, ncols), 0))
            z = jnp.where(causal, z, -jnp.inf)
        m_old = m_ref[rows, :]
        zmax = _lane_group_reduce(z, jnp.maximum)
        m_new = jnp.maximum(m_old, jnp.broadcast_to(jnp.max(zmax, axis=1, keepdims=True), zmax.shape))
        p = jnp.concatenate([jnp.exp2(z[:, c:c + V7X_LANES] - m_new)
                             for c in range(0, ncols, V7X_LANES)], axis=1)
        alpha = jnp.exp2(m_old - m_new)
        l_ref[rows, :] = alpha * l_ref[rows, :] + _lane_group_reduce(p, jnp.add)
        acc_ref[rows, :] = alpha * acc_ref[rows, :] + jnp.dot(p.astype(BF16), vb,
                                                            preferred_element_type=F32)
        m_ref[rows, :] = m_new

    def q_block(qi, _):
        qs = _aligned(qi * ATTN_Q_ROWS, ATTN_Q_ROWS)
        acc_ref[...] = jnp.zeros_like(acc_ref)
        l_ref[...] = jnp.zeros_like(l_ref)
        m_ref[...] = jnp.full(m_ref.shape, -jnp.inf, F32)
        for row0, nrows, col0, ncols in _diag_tiles():
            tile(qs, row0, nrows, qs + col0, ncols, True)

        def body(n, _):
            ks = _aligned(n * ATTN_K_COLS, ATTN_K_COLS)
            for row0 in range(0, ATTN_Q_ROWS, ATTN_Q_ROWS // ATTN_ROW_SPLIT):
                tile(qs, row0, ATTN_Q_ROWS // ATTN_ROW_SPLIT, ks, ATTN_K_COLS, False)
            return 0

        _unrolled(qi * (ATTN_Q_ROWS // ATTN_K_COLS), body)
        denom = jnp.sum(l_ref[...], axis=1, keepdims=True)
        o_ref[pl.ds(qs, ATTN_Q_ROWS), :] = (acc_ref[...] / denom).astype(o_ref.dtype)
        return 0

    _unrolled(seq // ATTN_Q_ROWS, q_block)


def _attn_kernel(q_ref, k_ref, v_ref, f_ref, o_ref, acc_ref, rest_ref, m_ref, l_ref, *, n_sb):
    head = pl.program_id(1)

    @pl.when(head < n_sb)
    def _():
        _stickbreak_head(q_ref, k_ref, v_ref, o_ref, acc_ref, rest_ref)

    @pl.when(head >= n_sb)
    def _():
        _forget_head(q_ref, k_ref, v_ref, f_ref, o_ref, acc_ref, m_ref, l_ref)


def _attention(qkv, f_rows, batch, seq, n_heads, n_sb):
    d = n_heads * HEAD_DIM
    kern = functools.partial(_attn_kernel, n_sb=n_sb)
    return pl.pallas_call(
        kern,
        grid=(batch, n_heads),
        in_specs=[
            pl.BlockSpec((None, seq, HEAD_DIM), lambda b, h: (b, 0, h)),
            pl.BlockSpec((None, seq, HEAD_DIM), lambda b, h: (b, 0, n_heads + h)),
            pl.BlockSpec((None, seq, HEAD_DIM), lambda b, h: (b, 0, 2 * n_heads + h)),
            pl.BlockSpec((None, None, 1, seq), lambda b, h: (b, jnp.maximum(h - n_sb, 0), 0, 0)),
        ],
        out_specs=pl.BlockSpec((None, seq, HEAD_DIM), lambda b, h: (b, 0, h)),
        out_shape=jax.ShapeDtypeStruct((batch, seq, d), BF16),
        scratch_shapes=[
            pltpu.VMEM((ATTN_Q_ROWS, HEAD_DIM), F32),
            pltpu.VMEM((ATTN_Q_ROWS, CUM_COLS), F32),
            pltpu.VMEM((ATTN_Q_ROWS, V7X_LANES), F32),
            pltpu.VMEM((ATTN_Q_ROWS, V7X_LANES), F32),
        ],
        compiler_params=_params("parallel", "parallel"),
        name="attention",
    )(qkv, qkv, qkv, f_rows)


def _next_stage(xn, nxt):
    kind = nxt[0]
    if kind == "mod":
        _, gain_ref, gain_row, nmod_ref, shift_row, h_ref = nxt
        h = (_rms(xn, gain_ref[gain_row:gain_row + 1, :]) * (1.0 + nmod_ref[shift_row + 1:shift_row + 2, :])
             + nmod_ref[shift_row:shift_row + 1, :])
        h_ref[...] = h.astype(h_ref.dtype)
    elif kind == "final":
        _, gain_ref, out_ref = nxt
        out_ref[...] = _rms(xn, gain_ref[...])


def _attn_out_kernel(o_ref, w_ref, x_ref, mod_ref, gain_ref, xn_ref, h_ref):
    y = jnp.dot(o_ref[...], w_ref[...], preferred_element_type=F32)
    xn = x_ref[...] + mod_ref[2:3, :] * y
    xn_ref[...] = xn
    _next_stage(xn, ("mod", gain_ref, 1, mod_ref, 3, h_ref))


def _attn_out(o2, w_out, wl, x2, mod, norm_gain, layer, seq):
    t, d = x2.shape
    tm = _tile(seq, 512)
    per_seq = seq // tm
    return pl.pallas_call(
        _attn_out_kernel,
        grid=(t // tm,),
        in_specs=[
            pl.BlockSpec((tm, d), lambda i: (i, 0)),
            pl.BlockSpec((None, d, d), lambda i: (wl, 0, 0), pipeline_mode=pl.Buffered(1)),
            pl.BlockSpec((tm, d), lambda i: (i, 0)),
            _mod_spec(layer, per_seq, d, 1),
            _gain_spec(layer, d, 1),
        ],
        out_specs=[
            pl.BlockSpec((tm, d), lambda i: (i, 0)),
            pl.BlockSpec((tm, d), lambda i: (i, 0)),
        ],
        out_shape=[jax.ShapeDtypeStruct((t, d), F32), jax.ShapeDtypeStruct((t, d), BF16)],
        compiler_params=_params("parallel"),
        name="attn_out",
    )(o2, w_out, x2, mod, norm_gain)


POOL_HALO = 16


def _pool_kernel(x_ref, halo_ref, w_ref, ps_ref, mod_ref, gain_ref, xn_ref, h_ref, hbuf, *, per_seq):
    i = pl.program_id(0)
    tm, d = x_ref.shape
    gw = d // len(POOL_WINDOWS)
    gain1 = gain_ref[0:1, :]
    shift1 = mod_ref[0:1, :]
    scale1 = mod_ref[1:2, :]
    x = x_ref[...]
    hbuf[POOL_HALO:, :] = _rms(x, gain1) * (1.0 + scale1) + shift1
    first = (i % per_seq) == 0

    @pl.when(first)
    def _():
        hbuf[:POOL_HALO, :] = jnp.zeros((POOL_HALO, d), F32)

    @pl.when(jnp.logical_not(first))
    def _():
        hbuf[:POOL_HALO, :] = _rms(halo_ref[...], gain1) * (1.0 + scale1) + shift1

    pos = (i % per_seq) * tm + lax.broadcasted_iota(jnp.int32, (tm, 1), 0)
    ys = []
    for g, win in enumerate(POOL_WINDOWS):
        lanes = slice(g * gw, (g + 1) * gw)
        cur = hbuf[POOL_HALO:, lanes]
        tot = cur
        for back in range(1, win):
            tot = tot + hbuf[POOL_HALO - back:POOL_HALO - back + tm, lanes]
        count = jnp.minimum(pos + 1, win).astype(F32)
        diff = tot / count - cur
        ys.append(jnp.dot(diff.astype(BF16), w_ref[g], preferred_element_type=F32))
    y = jnp.concatenate(ys, axis=1) * ps_ref[...]
    xn = x + mod_ref[2:3, :] * y
    xn_ref[...] = xn
    _next_stage(xn, ("mod", gain_ref, 1, mod_ref, 3, h_ref))


def _pool_layer(x2, w_pool, pool_scale, wl, mod, norm_gain, layer, seq):
    t, d = x2.shape
    tm = _tile(seq, 512)
    per_seq = seq // tm
    halo_per_tile = tm // POOL_HALO
    _, ng, gw, _ = w_pool.shape
    kern = functools.partial(_pool_kernel, per_seq=per_seq)
    return pl.pallas_call(
        kern,
        grid=(t // tm,),
        in_specs=[
            pl.BlockSpec((tm, d), lambda i: (i, 0)),
            pl.BlockSpec((POOL_HALO, d), lambda i: (jnp.maximum(i * halo_per_tile - 1, 0), 0)),
            pl.BlockSpec((None, ng, gw, gw), lambda i: (wl, 0, 0, 0), pipeline_mode=pl.Buffered(1)),
            pl.BlockSpec((None, 1, d), lambda i: (wl, 0, 0)),
            _mod_spec(layer, per_seq, d, 1),
            _gain_spec(layer, d, 1),
        ],
        out_specs=[
            pl.BlockSpec((tm, d), lambda i: (i, 0)),
            pl.BlockSpec((tm, d), lambda i: (i, 0)),
        ],
        out_shape=[jax.ShapeDtypeStruct((t, d), F32), jax.ShapeDtypeStruct((t, d), BF16)],
        scratch_shapes=[pltpu.VMEM((POOL_HALO + tm, d), F32)],
        compiler_params=_params("parallel"),
        name="pool_mixer",
    )(x2, x2, w_pool, pool_scale, mod, norm_gain)


CONV_PAD = V7X_SUBLANES
FFN_UP_CHUNK = 256


def _ffn_up_kernel(h_ref, wa_ref, wg_ref, cwa_ref, cwg_ref, cba_ref, cbg_ref, o_ref,
                   wab_ref, wgb_ref, ta_ref, tg_ref, *, per_seq):
    i = pl.program_id(1)
    tm = h_ref.shape[0]
    chunk = min(FFN_UP_CHUNK, tm)

    @pl.when(i == 0)
    def _():
        wab_ref[...] = wa_ref[...].astype(BF16)
        wgb_ref[...] = wg_ref[...].astype(BF16)

    @pl.when((i % per_seq) == 0)
    def _():
        ta_ref[...] = jnp.zeros_like(ta_ref)
        tg_ref[...] = jnp.zeros_like(tg_ref)

    def conv(tail, u, cw_ref, cb_ref):
        ext = jnp.concatenate([tail, u], axis=0)
        y = ext * cw_ref[0:1, :]
        for tap in range(1, CONV_WIDTH):
            y = pltpu.roll(y, 1, axis=0) + ext * cw_ref[tap:tap + 1, :]
        return y[CONV_PAD:, :] + cb_ref[...]

    tail_a = ta_ref[...]
    tail_g = tg_ref[...]
    for r0 in range(0, tm, chunk):
        hc = h_ref[r0:r0 + chunk, :]
        ua = jnp.dot(hc, wab_ref[...], preferred_element_type=F32)
        ug = jnp.dot(hc, wgb_ref[...], preferred_element_type=F32)
        a = conv(tail_a, ua, cwa_ref, cba_ref)
        g = conv(tail_g, ug, cwg_ref, cbg_ref)
        o_ref[r0:r0 + chunk, :] = ((g * _sigmoid(g)) * a).astype(o_ref.dtype)
        tail_a = ua[chunk - CONV_PAD:, :]
        tail_g = ug[chunk - CONV_PAD:, :]
    ta_ref[...] = tail_a
    tg_ref[...] = tail_g


def _ffn_up(h, w_up, conv_w, conv_b, layer, seq):
    t, d = h.shape
    f2 = w_up.shape[2]
    dff = f2 // 2
    tm = _tile(seq, 1024)
    per_seq = seq // tm
    tn = _tile(dff, 512)
    nj = dff // tn
    kern = functools.partial(_ffn_up_kernel, per_seq=per_seq)
    return pl.pallas_call(
        kern,
        grid=(nj, t // tm),
        in_specs=[
            pl.BlockSpec((tm, d), lambda j, i: (i, 0)),
            pl.BlockSpec((None, d, tn), lambda j, i: (layer, 0, j)),
            pl.BlockSpec((None, d, tn), lambda j, i: (layer, 0, nj + j)),
            pl.BlockSpec((None, CONV_WIDTH, tn), lambda j, i: (layer, 0, j)),
            pl.BlockSpec((None, CONV_WIDTH, tn), lambda j, i: (layer, 0, nj + j)),
            pl.BlockSpec((None, 1, tn), lambda j, i: (layer, 0, j)),
            pl.BlockSpec((None, 1, tn), lambda j, i: (layer, 0, nj + j)),
        ],
        out_specs=pl.BlockSpec((tm, tn), lambda j, i: (i, j)),
        out_shape=jax.ShapeDtypeStruct((t, dff), BF16),
        scratch_shapes=[pltpu.VMEM((d, tn), BF16), pltpu.VMEM((d, tn), BF16),
                        pltpu.VMEM((CONV_PAD, tn), F32), pltpu.VMEM((CONV_PAD, tn), F32)],
        compiler_params=_params("parallel", "arbitrary"),
        name="ffn_up",
    )(h, w_up, w_up, conv_w, conv_w, conv_b, conv_b)


def _ffn_down_kernel(*refs, nn, tn, nxt_kind):
    if nxt_kind == "mod":
        a_ref, w_ref, x_ref, mod_ref, gain_ref, nmod_ref, xn_ref, h_ref = refs
        nxt = ("mod", gain_ref, 0, nmod_ref, 0, h_ref)
    elif nxt_kind == "final":
        a_ref, w_ref, x_ref, mod_ref, gain_ref, xn_ref, out_ref = refs
        nxt = ("final", gain_ref, out_ref)
    else:
        a_ref, w_ref, x_ref, mod_ref, xn_ref = refs
        nxt = ("none",)
    n = pl.program_id(1)
    cols = pl.ds(_aligned(n * tn, tn), tn)
    y = jnp.dot(a_ref[...], w_ref[...], preferred_element_type=F32)
    xn_ref[:, cols] = x_ref[:, cols] + mod_ref[5:6, cols] * y

    if nxt_kind != "none":
        @pl.when(n == nn - 1)
        def _():
            _next_stage(xn_ref[...], nxt)


def _ffn_down(act, w_down, x2, mod, layer, seq, nxt_kind, gain=None):
    t, d = x2.shape
    dff = act.shape[1]
    tm = _tile(seq, 512)
    per_seq = seq // tm
    tn = _tile(d, 512)
    nn = d // tn
    row = lambda i, n: (i, 0)
    in_specs = [
        pl.BlockSpec((tm, dff), row),
        pl.BlockSpec((None, dff, tn), lambda i, n: (layer, 0, n)),
        pl.BlockSpec((tm, d), row),
        _mod_spec(layer, per_seq, d, 2),
    ]
    args = [act, w_down, x2, mod]
    out_specs = [pl.BlockSpec((tm, d), row)]
    out_shape = [jax.ShapeDtypeStruct((t, d), F32)]
    if nxt_kind == "mod":
        in_specs += [_gain_spec(layer + 1, d, 2), _mod_spec(layer + 1, per_seq, d, 2)]
        args += [gain, mod]
        out_specs.append(pl.BlockSpec((tm, d), row))
        out_shape.append(jax.ShapeDtypeStruct((t, d), BF16))
    elif nxt_kind == "final":
        in_specs.append(pl.BlockSpec((1, d), lambda i, n: (0, 0)))
        args.append(gain)
        out_specs.append(pl.BlockSpec((tm, d), row))
        out_shape.append(jax.ShapeDtypeStruct((t, d), F32))
    kern = functools.partial(_ffn_down_kernel, nn=nn, tn=tn, nxt_kind=nxt_kind)
    return pl.pallas_call(
        kern,
        grid=(t // tm, nn),
        in_specs=in_specs,
        out_specs=out_specs,
        out_shape=out_shape,
        compiler_params=_params("parallel", "arbitrary"),
        name="ffn_down",
    )(*args)


def kernel(x, c, w_mod, b_mod, norm_gain, w_attn_in, b_forget, w_attn_out, w_pool, pool_scale,
           w_up, conv_w, conv_b, w_down, final_gain):
    batch, seq, d = x.shape
    depth = w_mod.shape[0]
    n_heads = d // HEAD_DIM
    n_fox = b_forget.shape[1]
    n_sb = n_heads - n_fox
    assert seq % ATTN_Q_ROWS == 0 and d % (len(POOL_WINDOWS) * V7X_LANES) == 0

    mod = _mod_all(c, w_mod, b_mod)
    w_in_t = jnp.swapaxes(w_attn_in, 1, 2)
    w_out_b = w_attn_out.astype(BF16)
    w_pool_b = w_pool.astype(BF16)
    w_down_b = w_down.astype(BF16)
    pool_scale3 = pool_scale.reshape(pool_scale.shape[0], 1, d)
    conv_b3 = conv_b.reshape(depth, 1, conv_b.shape[1])

    x2 = x.reshape(batch * seq, d)
    h = _prenorm(x2, norm_gain, mod, 0, seq)
    out = None
    for l in range(depth):
        i = l // 2
        if l % 2 == 0:
            qkv = _qkv_proj(h, w_in_t, i, 3 * d, d, LOG2E * HEAD_DIM ** -0.5)
            wf = jnp.pad(w_in_t[i, 3 * d:, :], ((0, V7X_LANES - n_fox), (0, 0)))
            bf = jnp.pad(b_forget[i], (0, V7X_LANES - n_fox)).reshape(1, V7X_LANES)
            fcum = _forget_cumsum(h, wf, bf, batch, seq)
            f_rows = fcum.reshape(batch, seq, V7X_LANES)[:, :, :n_fox].transpose(0, 2, 1)
            f_rows = f_rows.reshape(batch, n_fox, 1, seq)
            o = _attention(qkv.reshape(batch, seq, 3 * d), f_rows, batch, seq, n_heads, n_sb)
            x2, h = _attn_out(o.reshape(batch * seq, d), w_out_b, i, x2, mod, norm_gain, l, seq)
        else:
            x2, h = _pool_layer(x2, w_pool_b, pool_scale3, i, mod, norm_gain, l, seq)
        act = _ffn_up(h, w_up, conv_w, conv_b3, l, seq)
        if l == depth - 1:
            x2, out = _ffn_down(act, w_down_b, x2, mod, l, seq, "final", gain=final_gain.reshape(1, d))
        elif (l + 1) % 2 == 0:
            x2, h = _ffn_down(act, w_down_b, x2, mod, l, seq, "mod", gain=norm_gain)
        else:
            (x2,) = _ffn_down(act, w_down_b, x2, mod, l, seq, "none")
    return out.reshape(batch, seq, d)
```
